```python
import jax, jax.numpy as jnp
from jax import lax
import numpy as np

D_MODEL = 2048
BATCH = 8
SEQ = 2048
DEPTH = 2
DEC_BATCH = 32
DEC_SEQ = 32
PAST_LEN = 2048

CHUNK = 64
N_MIXERS = 2
N_A_LAYERS = (DEPTH + N_MIXERS - 1) // N_MIXERS
N_B_LAYERS = DEPTH // N_MIXERS

HG_KDIM = 128
HG_HEADS = D_MODEL // HG_KDIM
HG_VDIM = D_MODEL // HG_HEADS
HG_KW = HG_HEADS * HG_KDIM
HG_VW = HG_HEADS * HG_VDIM

FOX_HDIM = 128
FOX_HEADS = D_MODEL // FOX_HDIM
FOX_WIDTH = FOX_HEADS * FOX_HDIM
FOX_QBLOCK = 128
FOX_FORGET_BIAS = 3.0

MEM_TOKENS = 256
MEM_HEADS = 4
MEM_HDIM = 128
MEM_WIDTH = MEM_HEADS * MEM_HDIM

N_EXPERTS = 32
TOP_K = 4
D_FF = D_MODEL
SWIGLU_ALPHA = 1.702
SWIGLU_LIMIT = 7.0
MOE_BLOCK = 256

DEEPNORM_ALPHA = (2 * DEPTH) ** 0.25
DEEPNORM_BETA = (8 * DEPTH) ** -0.25
LN_EPS = 1e-5
RMS_EPS = 1e-6
NEG_INF = -1e30

kernel_name = "hgrn2_fox_memxattn_moe_stream_step"


def _layernorm(x, g, b):
    xf = x.astype(jnp.float32)
    mu = jnp.mean(xf, axis=-1, keepdims=True)
    var = jnp.mean(jnp.square(xf - mu), axis=-1, keepdims=True)
    return ((xf - mu) * lax.rsqrt(var + LN_EPS) * g.astype(jnp.float32) + b.astype(jnp.float32)).astype(x.dtype)


def _hgrn2_chunk(S, inp):
    q, k, v, lf = inp
    L = q.shape[1]
    b = jnp.cumsum(lf, axis=1)
    q_dec = q * jnp.exp(b)
    k_inv = k * jnp.exp(-b)
    causal = jnp.tril(jnp.ones((L, L), dtype=bool))
    a = jnp.where(causal, jnp.einsum("blhk,bshk->bhls", q_dec, k_inv), 0.0)
    o = jnp.einsum("blhk,bhkv->blhv", q_dec, S) + jnp.einsum("bhls,bshv->blhv", a, v)
    b_end = b[:, -1:]
    k_end = k * jnp.exp(b_end - b)
    S = S * jnp.exp(b_end[:, 0])[..., None] + jnp.einsum("bshk,bshv->bhkv", k_end, v)
    return S, o


def _hgrn2_mix(x, S0, w_qfig, lb, norm_g, w_o):
    B, T, _ = x.shape
    proj = jnp.einsum("btd,de->bte", x, w_qfig).astype(jnp.float32)
    zq = proj[..., :HG_KW]
    zf = proj[..., HG_KW:2 * HG_KW]
    zi = proj[..., 2 * HG_KW:2 * HG_KW + HG_VW]
    zg = proj[..., 2 * HG_KW + HG_VW:]
    q = jax.nn.silu(zq)
    f = lb + (1.0 - lb) * jax.nn.sigmoid(zf)
    k = (1.0 - lb) * jax.nn.sigmoid(-zf)
    logf = jnp.log(f)
    L = min(T, CHUNK)
    n = T // L

    def blocks(a, d):
        return jnp.swapaxes(a.reshape(B, n, L, HG_HEADS, d), 0, 1)

    S_fin, o = lax.scan(_hgrn2_chunk, S0.astype(jnp.float32),
                        (blocks(q, HG_KDIM), blocks(k, HG_KDIM), blocks(zi, HG_VDIM), blocks(logf, HG_KDIM)))
    o = jnp.swapaxes(o, 0, 1).reshape(B, T, HG_HEADS, HG_VDIM)
    o = o * lax.rsqrt(jnp.mean(jnp.square(o), axis=-1, keepdims=True) + RMS_EPS) \
        * norm_g.astype(jnp.float32).reshape(HG_HEADS, HG_VDIM)
    o = o.reshape(B, T, HG_VW) * jax.nn.silu(zg)
    y = jnp.einsum("btf,fd->btd", o.astype(x.dtype), w_o)
    return y, S_fin


def _fox_project(x, w_qkv, w_f, b_f):
    B, T, _ = x.shape
    qkv = jnp.einsum("btd,de->bte", x, w_qkv).reshape(B, T, 3, FOX_HEADS, FOX_HDIM)
    logf = jax.nn.log_sigmoid((jnp.einsum("btd,dh->bth", x, w_f) + b_f).astype(jnp.float32))
    return qkv[:, :, 0], qkv[:, :, 1], qkv[:, :, 2], logf


def _fox_attend(q, k, v, c_q, c_k, q_pos, k_pos):
    s = jnp.einsum("bqhe,bshe->bhqs", q, k).astype(jnp.float32) * (FOX_HDIM ** -0.5)
    s = s + jnp.swapaxes(c_q, 1, 2)[..., :, None] - jnp.swapaxes(c_k, 1, 2)[..., None, :]
    s = jnp.where(q_pos[:, None] >= k_pos[None, :], s, NEG_INF)
    p = jax.nn.softmax(s, axis=-1)
    return jnp.einsum("bhqs,bshe->bqhe", p.astype(v.dtype), v)


def _fox_prompt(x, w_qkv, w_f, b_f, w_o):
    B, T, _ = x.shape
    q, k, v, logf = _fox_project(x, w_qkv, w_f, b_f)
    c = jnp.cumsum(logf, axis=1)
    k_pos = jnp.arange(T)

    def q_block(i):
        start = i * FOX_QBLOCK
        qb = lax.dynamic_slice_in_dim(q, start, FOX_QBLOCK, axis=1)
        cb = lax.dynamic_slice_in_dim(c, start, FOX_QBLOCK, axis=1)
        return _fox_attend(qb, k, v, cb, c, start + jnp.arange(FOX_QBLOCK), k_pos)

    o = lax.map(q_block, jnp.arange(T // FOX_QBLOCK))
    o = jnp.moveaxis(o, 0, 1).reshape(B, T, FOX_WIDTH)
    y = jnp.einsum("btf,fd->btd", o, w_o)
    return y, k, v, logf.astype(x.dtype)


def _fox_sample(x, cache_k, cache_v, cache_logf, w_qkv, w_f, b_f, w_o):
    B, S, _ = x.shape
    P = cache_k.shape[1]
    q, k, v, logf = _fox_project(x, w_qkv, w_f, b_f)
    k_all = jnp.concatenate([cache_k.astype(k.dtype), k], axis=1)
    v_all = jnp.concatenate([cache_v.astype(v.dtype), v], axis=1)
    c_all = jnp.cumsum(jnp.concatenate([cache_logf.astype(jnp.float32), logf], axis=1), axis=1)
    o = _fox_attend(q, k_all, v_all, c_all[:, P:], c_all, P + jnp.arange(S), jnp.arange(P + S))
    y = jnp.einsum("btf,fd->btd", o.reshape(B, S, FOX_WIDTH), w_o)
    return y, k, v, logf.astype(cache_logf.dtype)


def _mem_project(mem, w_kv):
    B, M, _ = mem.shape
    kv = jnp.einsum("bmd,de->bme", mem, w_kv).reshape(B, M, 2, MEM_HEADS, MEM_HDIM)
    return kv[:, :, 0], kv[:, :, 1]


def _cross_attend(x, mem_k, mem_v, w_q, w_o):
    B, T, _ = x.shape
    q = jnp.einsum("btd,de->bte", x, w_q).reshape(B, T, MEM_HEADS, MEM_HDIM)
    s = jnp.einsum("bthe,bmhe->bhtm", q, mem_k.astype(q.dtype)).astype(jnp.float32) * (MEM_HDIM ** -0.5)
    p = jax.nn.softmax(s, axis=-1)
    o = jnp.einsum("bhtm,bmhe->bthe", p.astype(x.dtype), mem_v.astype(x.dtype))
    return jnp.einsum("btf,fd->btd", o.reshape(B, T, MEM_WIDTH), w_o)


def _moe(x, w_router, b_router, w_gu, b_gu, w_down, b_down):
    xt = x.reshape(-1, D_MODEL)
    T = xt.shape[0]
    A = T * TOP_K
    logits = jnp.einsum("td,de->te", xt, w_router).astype(jnp.float32) + b_router.astype(jnp.float32)
    top_logit, top_e = lax.top_k(logits, TOP_K)
    gate = jax.nn.softmax(top_logit, axis=-1).astype(x.dtype).reshape(A)
    flat_e = top_e.reshape(A)
    counts = jnp.zeros((N_EXPERTS,), jnp.int32).at[flat_e].add(1)
    padded = (counts + MOE_BLOCK - 1) // MOE_BLOCK * MOE_BLOCK
    pad_end = jnp.cumsum(padded)
    pad_start = pad_end - padded
    grp_start = jnp.cumsum(counts) - counts
    order = jnp.argsort(flat_e)
    e_sorted = flat_e[order]
    dest = pad_start[e_sorted] + jnp.arange(A) - grp_start[e_sorted]
    tok_sorted = order // TOP_K
    n_blocks = -(-A // MOE_BLOCK) + N_EXPERTS
    buf = jnp.zeros((n_blocks * MOE_BLOCK, D_MODEL), x.dtype).at[dest].set(xt[tok_sorted])
    block_e = jnp.minimum(jnp.searchsorted(pad_end, jnp.arange(n_blocks) * MOE_BLOCK, side="right"),
                          N_EXPERTS - 1)

    def expert_block(args):
        xb, e = args
        gu = xb @ w_gu[e] + b_gu[e]
        x_glu = jnp.minimum(gu[:, :D_FF], SWIGLU_LIMIT)
        x_lin = jnp.clip(gu[:, D_FF:], -SWIGLU_LIMIT, SWIGLU_LIMIT)
        h = x_glu * jax.nn.sigmoid(SWIGLU_ALPHA * x_glu) * (x_lin + 1.0)
        return h @ w_down[e] + b_down[e]

    out = lax.map(expert_block, (buf.reshape(n_blocks, MOE_BLOCK, D_MODEL), block_e))
    rows = out.reshape(-1, D_MODEL)[dest] * gate[order][:, None]
    y = jax.ops.segment_sum(rows, tok_sorted, num_segments=T)
    return y.reshape(x.shape)


def setup_inputs(seed: int = 0) -> dict:
    key = jax.random.key(seed)
    ks = jax.random.split(key, 32)
    D = D_MODEL
    s_in = D ** -0.5

    def nrm(k, shape, scale):
        return jax.random.normal(k, shape, jnp.float32) * scale

    return {
        "x_prompt": nrm(ks[0], (BATCH, SEQ, D), 1.0),
        "x_sample": nrm(ks[1], (DEC_BATCH, DEC_SEQ, D), 1.0),
        "state_hgrn": nrm(ks[2], (N_A_LAYERS, DEC_BATCH, HG_HEADS, HG_KDIM, HG_VDIM), 0.5),
        "cache_fox_k": nrm(ks[3], (N_B_LAYERS, DEC_BATCH, PAST_LEN, FOX_HEADS, FOX_HDIM), 1.0),
        "cache_fox_v": nrm(ks[4], (N_B_LAYERS, DEC_BATCH, PAST_LEN, FOX_HEADS, FOX_HDIM), 1.0),
        "cache_fox_logf": jax.nn.log_sigmoid(FOX_FORGET_BIAS + nrm(ks[5], (N_B_LAYERS, DEC_BATCH, PAST_LEN, FOX_HEADS), 1.0)),
        "cache_mem_k": nrm(ks[6], (DEPTH, DEC_BATCH, MEM_TOKENS, MEM_HEADS, MEM_HDIM), 1.0),
        "cache_mem_v": nrm(ks[7], (DEPTH, DEC_BATCH, MEM_TOKENS, MEM_HEADS, MEM_HDIM), 1.0),
        "mem_prompt": nrm(ks[8], (BATCH, MEM_TOKENS, D), 1.0),
        "hg_w_qfig": nrm(ks[9], (N_A_LAYERS, D, 2 * HG_KW + 2 * HG_VW), s_in),
        "hg_lb": nrm(ks[10], (N_A_LAYERS + 1, HG_KW), 0.5),
        "hg_norm_g": 1.0 + nrm(ks[11], (N_A_LAYERS, HG_VW), 0.02),
        "hg_w_o": nrm(ks[12], (N_A_LAYERS, HG_VW, D), HG_VW ** -0.5 * DEEPNORM_BETA),
        "fox_w_qkv": nrm(ks[13], (N_B_LAYERS, D, 3 * FOX_WIDTH), s_in),
        "fox_w_f": nrm(ks[14], (N_B_LAYERS, D, FOX_HEADS), s_in),
        "fox_b_f": FOX_FORGET_BIAS + nrm(ks[15], (N_B_LAYERS, FOX_HEADS), 0.1),
        "fox_w_o": nrm(ks[16], (N_B_LAYERS, FOX_WIDTH, D), FOX_WIDTH ** -0.5 * DEEPNORM_BETA),
        "mem_w_q": nrm(ks[17], (DEPTH, D, MEM_WIDTH), s_in),
        "mem_w_kv": nrm(ks[18], (DEPTH, D, 2 * MEM_WIDTH), s_in),
        "mem_w_o": nrm(ks[19], (DEPTH, MEM_WIDTH, D), MEM_WIDTH ** -0.5 * DEEPNORM_BETA),
        "ln_g": 1.0 + nrm(ks[20], (DEPTH, 3, D), 0.02),
        "ln_b": nrm(ks[21], (DEPTH, 3, D), 0.02),
        "moe_w_router": nrm(ks[22], (DEPTH, D, N_EXPERTS), s_in),
        "moe_b_router": nrm(ks[23], (DEPTH, N_EXPERTS), 0.01),
        "moe_w_gu": nrm(ks[24], (DEPTH, N_EXPERTS, D, 2 * D_FF), s_in),
        "moe_b_gu": nrm(ks[25], (DEPTH, N_EXPERTS, 2 * D_FF), 0.01),
        "moe_w_down": nrm(ks[26], (DEPTH, N_EXPERTS, D_FF, D), D_FF ** -0.5 * DEEPNORM_BETA),
        "moe_b_down": nrm(ks[27], (DEPTH, N_EXPERTS, D), 0.01),
    }


def reference(x_prompt, x_sample, state_hgrn, cache_fox_k, cache_fox_v, cache_fox_logf,
              cache_mem_k, cache_mem_v, mem_prompt, hg_w_qfig, hg_lb, hg_norm_g, hg_w_o,
              fox_w_qkv, fox_w_f, fox_b_f, fox_w_o, mem_w_q, mem_w_kv, mem_w_o, ln_g, ln_b,
              moe_w_router, moe_b_router, moe_w_gu, moe_b_gu, moe_w_down, moe_b_down):
    lower_bounds = jnp.cumsum(jax.nn.softmax(hg_lb.astype(jnp.float32), axis=0), axis=0)
    yp, ys = x_prompt, x_sample
    p_hg, p_fk, p_fv, p_fl, p_mk, p_mv = [], [], [], [], [], []
    s_hg, s_fk, s_fv, s_fl = [], [], [], []
    for i in range(DEPTH):
        j = i // N_MIXERS
        if i % N_MIXERS == 0:
            S0 = jnp.zeros((yp.shape[0], HG_HEADS, HG_KDIM, HG_VDIM), jnp.float32)
            mix_p, st_p = _hgrn2_mix(yp, S0, hg_w_qfig[j], lower_bounds[j], hg_norm_g[j], hg_w_o[j])
            mix_s, st_s = _hgrn2_mix(ys, state_hgrn[j], hg_w_qfig[j], lower_bounds[j], hg_norm_g[j], hg_w_o[j])
            p_hg.append(st_p.astype(yp.dtype))
            s_hg.append(st_s.astype(state_hgrn.dtype))
        else:
            mix_p, kp, vp, lfp = _fox_prompt(yp, fox_w_qkv[j], fox_w_f[j], fox_b_f[j], fox_w_o[j])
            mix_s, ks_, vs_, lfs = _fox_sample(ys, cache_fox_k[j], cache_fox_v[j], cache_fox_logf[j],
                                               fox_w_qkv[j], fox_w_f[j], fox_b_f[j], fox_w_o[j])
            p_fk.append(kp); p_fv.append(vp); p_fl.append(lfp)
            s_fk.append(ks_); s_fv.append(vs_); s_fl.append(lfs)
        yp = _layernorm(DEEPNORM_ALPHA * yp + mix_p, ln_g[i, 0], ln_b[i, 0])
        ys = _layernorm(DEEPNORM_ALPHA * ys + mix_s, ln_g[i, 0], ln_b[i, 0])
        mk_p, mv_p = _mem_project(mem_prompt, mem_w_kv[i])
        p_mk.append(mk_p); p_mv.append(mv_p)
        yp = _layernorm(DEEPNORM_ALPHA * yp + _cross_attend(yp, mk_p, mv_p, mem_w_q[i], mem_w_o[i]),
                        ln_g[i, 1], ln_b[i, 1])
        ys = _layernorm(DEEPNORM_ALPHA * ys + _cross_attend(ys, cache_mem_k[i], cache_mem_v[i], mem_w_q[i], mem_w_o[i]),
                        ln_g[i, 1], ln_b[i, 1])
        moe_w = (moe_w_router[i], moe_b_router[i], moe_w_gu[i], moe_b_gu[i], moe_w_down[i], moe_b_down[i])
        yp = _layernorm(DEEPNORM_ALPHA * yp + _moe(yp, *moe_w), ln_g[i, 2], ln_b[i, 2])
        ys = _layernorm(DEEPNORM_ALPHA * ys + _moe(ys, *moe_w), ln_g[i, 2], ln_b[i, 2])
    return (yp, ys,
            jnp.stack(p_hg), jnp.stack(p_fk), jnp.stack(p_fv), jnp.stack(p_fl),
            jnp.stack(p_mk), jnp.stack(p_mv),
            jnp.stack(s_hg), jnp.stack(s_fk), jnp.stack(s_fv), jnp.stack(s_fl))
```

```python
import functools

import jax
import jax.numpy as jnp
from jax import lax
from jax.experimental import pallas as pl
from jax.experimental.pallas import tpu as pltpu

F32 = jnp.float32
BF16 = jnp.bfloat16

D_MODEL = 2048
BATCH = 8
SEQ = 2048
DEPTH = 2
DEC_BATCH = 32
DEC_SEQ = 32
PAST_LEN = 2048
CHUNK = 64
N_MIXERS = 2

HG_KDIM = 128
HG_HEADS = D_MODEL // HG_KDIM
HG_VDIM = D_MODEL // HG_HEADS
HG_KW = HG_HEADS * HG_KDIM
HG_VW = HG_HEADS * HG_VDIM

FOX_HDIM = 128
FOX_HEADS = D_MODEL // FOX_HDIM
FOX_WIDTH = FOX_HEADS * FOX_HDIM

MEM_TOKENS = 256
MEM_HEADS = 4
MEM_HDIM = 128
MEM_WIDTH = MEM_HEADS * MEM_HDIM

N_EXPERTS = 32
TOP_K = 4
D_FF = D_MODEL
SWIGLU_ALPHA = 1.702
SWIGLU_LIMIT = 7.0

DEEPNORM_ALPHA = (2 * DEPTH) ** 0.25
LN_EPS = 1e-5
RMS_EPS = 1e-6
NEG_INF = -1e30

N_PROMPT = BATCH * SEQ
N_SAMPLE = DEC_BATCH * DEC_SEQ
N_TOK = N_PROMPT + N_SAMPLE

V7X_LANES = 128
VMEM_LIMIT = 56 * 1024 * 1024

MOE_TM = 512
MOE_ROWS = N_TOK * TOP_K
MOE_BLOCKS = MOE_ROWS // MOE_TM + N_EXPERTS
MOE_TF = 512
MOE_TN = 1024


def _cparams(*sem):
    return pltpu.CompilerParams(dimension_semantics=sem, vmem_limit_bytes=VMEM_LIMIT)


def _sigmoid(x):
    return 1.0 / (1.0 + jnp.exp(-x))


def _layernorm_rows(z, g, b):
    mu = jnp.mean(z, axis=-1, keepdims=True)
    zc = z - mu
    var = jnp.mean(zc * zc, axis=-1, keepdims=True)
    return zc * lax.rsqrt(var + LN_EPS) * g + b


def _mm_kernel(x_ref, w_ref, o_ref):
    o_ref[...] = jnp.dot(x_ref[...].astype(BF16), w_ref[...],
                         preferred_element_type=F32).astype(o_ref.dtype)


def _matmul(x, w, *, rows, row_off, tm, tn, out_dtype):
    k = x.shape[1]
    n = w.shape[1]
    off = row_off // tm
    return pl.pallas_call(
        _mm_kernel,
        grid=(rows // tm, n // tn),
        in_specs=[pl.BlockSpec((tm, k), lambda i, j: (i + off, 0)),
                  pl.BlockSpec((k, tn), lambda i, j: (0, j))],
        out_specs=pl.BlockSpec((tm, tn), lambda i, j: (i, j)),
        out_shape=jax.ShapeDtypeStruct((rows, n), out_dtype),
        compiler_params=_cparams("parallel", "parallel"),
        name="matmul",
    )(x, w)


def _outproj_ln_kernel(o_ref, w_ref, x_ref, g_ref, b_ref, y_ref):
    mix = jnp.dot(o_ref[...].astype(BF16), w_ref[...], preferred_element_type=F32)
    z = DEEPNORM_ALPHA * x_ref[...] + mix
    y_ref[...] = _layernorm_rows(z, g_ref[...], b_ref[...])


def _outproj_ln(o, w, x, g, b, *, tm=512):
    rows, k = o.shape
    d = w.shape[1]
    return pl.pallas_call(
        _outproj_ln_kernel,
        grid=(rows // tm,),
        in_specs=[pl.BlockSpec((tm, k), lambda i: (i, 0)),
                  pl.BlockSpec((k, d), lambda i: (0, 0)),
                  pl.BlockSpec((tm, d), lambda i: (i, 0)),
                  pl.BlockSpec((1, d), lambda i: (0, 0)),
                  pl.BlockSpec((1, d), lambda i: (0, 0))],
        out_specs=pl.BlockSpec((tm, d), lambda i: (i, 0)),
        out_shape=jax.ShapeDtypeStruct((rows, d), F32),
        compiler_params=_cparams("parallel"),
        name="outproj_ln",
    )(o, w, x, g.reshape(1, d), b.reshape(1, d))


def _residual_ln_kernel(x_ref, y_ref, g_ref, b_ref, o_ref):
    z = DEEPNORM_ALPHA * x_ref[...] + y_ref[...]
    o_ref[...] = _layernorm_rows(z, g_ref[...], b_ref[...])


def _residual_ln(x, y, g, b, *, rows, row_off, tm=512):
    d = x.shape[1]
    off = row_off // tm
    return pl.pallas_call(
        _residual_ln_kernel,
        grid=(rows // tm,),
        in_specs=[pl.BlockSpec((tm, d), lambda i: (i + off, 0)),
                  pl.BlockSpec((tm, d), lambda i: (i + off, 0)),
                  pl.BlockSpec((1, d), lambda i: (0, 0)),
                  pl.BlockSpec((1, d), lambda i: (0, 0))],
        out_specs=pl.BlockSpec((tm, d), lambda i: (i, 0)),
        out_shape=jax.ShapeDtypeStruct((rows, d), F32),
        compiler_params=_cparams("parallel"),
        name="residual_ln",
    )(x, y, g.reshape(1, d), b.reshape(1, d))


def _hgrn_kernel(*refs, L, has_s0):
    if has_s0:
        zq_ref, zf_ref, zi_ref, zg_ref, lb_ref, ng_ref, s0_ref, og_ref, sfin_ref, st_scr = refs
    else:
        zq_ref, zf_ref, zi_ref, zg_ref, lb_ref, ng_ref, og_ref, sfin_ref, st_scr = refs
    c = pl.program_id(1)

    @pl.when(c == 0)
    def _():
        if has_s0:
            for h in range(HG_HEADS):
                st_scr[h] = s0_ref[h].T
        else:
            st_scr[...] = jnp.zeros_like(st_scr)

    lb = lb_ref[...]
    zf = zf_ref[...]
    f = lb + (1.0 - lb) * _sigmoid(zf)
    kk = (1.0 - lb) * _sigmoid(-zf)
    lf = jnp.log(f)
    row = lax.broadcasted_iota(jnp.int32, (L, L), 0)
    col = lax.broadcasted_iota(jnp.int32, (L, L), 1)
    causal = row >= col
    b = jnp.dot(causal.astype(F32), lf, precision=lax.Precision.HIGHEST, preferred_element_type=F32)
    b_end = b[L - 1:L, :]
    zq = zq_ref[...]
    q_dec = (zq * _sigmoid(zq)) * jnp.exp(b)
    k_inv = kk * jnp.exp(-b)
    k_end = kk * jnp.exp(b_end - b)
    e_end = jnp.exp(b_end)
    v = zi_ref[...]
    zg = zg_ref[...]
    gate = zg * _sigmoid(zg)
    ng = ng_ref[...]
    nt = (((1,), (1,)), ((), ()))
    tn = (((0,), (0,)), ((), ()))
    for h in range(HG_HEADS):
        sl = slice(h * HG_KDIM, (h + 1) * HG_KDIM)
        qd = q_dec[:, sl].astype(BF16)
        ki = k_inv[:, sl].astype(BF16)
        ke = k_end[:, sl].astype(BF16)
        vh = v[:, sl].astype(BF16)
        a = lax.dot_general(qd, ki, nt, preferred_element_type=F32)
        a = jnp.where(causal, a, 0.0)
        st = st_scr[h]
        o = (lax.dot_general(qd, st.astype(BF16), nt, preferred_element_type=F32)
             + jnp.dot(a.astype(BF16), vh, preferred_element_type=F32))
        st_scr[h] = st * e_end[:, sl] + lax.dot_general(vh, ke, tn, preferred_element_type=F32)
        ms = jnp.mean(o * o, axis=-1, keepdims=True)
        on = o * lax.rsqrt(ms + RMS_EPS) * ng[:, sl]
        og_ref[:, sl] = (on * gate[:, sl]).astype(og_ref.dtype)

    @pl.when(c == pl.num_programs(1) - 1)
    def _():
        for h in range(HG_HEADS):
            sfin_ref[h] = st_scr[h].T


def _hgrn_mix(proj, lb, ng, s0, *, batch, seq, row_off):
    L = min(seq, CHUNK)
    n = seq // L
    off = row_off // L
    has_s0 = s0 is not None

    def col_spec(cb):
        return pl.BlockSpec((L, HG_KW), lambda b, c: (off + b * n + c, cb))

    vec_spec = pl.BlockSpec((1, HG_KW), lambda b, c: (0, 0))
    st_spec = pl.BlockSpec((None, HG_HEADS, HG_VDIM, HG_KDIM), lambda b, c: (b, 0, 0, 0))
    in_specs = [col_spec(0), col_spec(1), col_spec(2), col_spec(3), vec_spec, vec_spec]
    args = [proj, proj, proj, proj, lb.reshape(1, HG_KW), ng.reshape(1, HG_VW)]
    if has_s0:
        in_specs.append(st_spec)
        args.append(s0)
    return pl.pallas_call(
        functools.partial(_hgrn_kernel, L=L, has_s0=has_s0),
        grid=(batch, n),
        in_specs=in_specs,
        out_specs=[pl.BlockSpec((L, HG_VW), lambda b, c: (b * n + c, 0)), st_spec],
        out_shape=[jax.ShapeDtypeStruct((batch * seq, HG_VW), BF16),
                   jax.ShapeDtypeStruct((batch, HG_HEADS, HG_KDIM, HG_VDIM), F32)],
        scratch_shapes=[pltpu.VMEM((HG_HEADS, HG_VDIM, HG_KDIM), F32)],
        compiler_params=_cparams("parallel", "arbitrary"),
        name="hgrn_mix",
    )(*args)


def _log_sigmoid(z):
    return jnp.minimum(z, 0.0) - jnp.log(1.0 + jnp.exp(-jnp.abs(z)))


def _fox_proj_kernel(x_ref, wq_ref, wk_ref, wv_ref, wf_ref, bf_ref, q_ref, k_ref, v_ref, lf_ref):
    xb = x_ref[...].astype(BF16)
    q_ref[...] = jnp.dot(xb, wq_ref[...], preferred_element_type=F32).astype(q_ref.dtype)
    k_ref[...] = jnp.dot(xb, wk_ref[...], preferred_element_type=F32)
    v_ref[...] = jnp.dot(xb, wv_ref[...], preferred_element_type=F32)

    @pl.when(pl.program_id(1) == 0)
    def _():
        z = jnp.dot(xb, wf_ref[...], preferred_element_type=F32) + bf_ref[...]
        lf_ref[...] = _log_sigmoid(z)


def _fox_project(x, w_qkv, w_f, b_f, *, rows, row_off, tm, tn=512):
    d = x.shape[1]
    off = row_off // tm
    ncb = FOX_WIDTH // tn

    def w_spec(part):
        return pl.BlockSpec((d, tn), lambda i, j: (0, part * ncb + j))

    out_spec = pl.BlockSpec((tm, tn), lambda i, j: (i, j))
    return pl.pallas_call(
        _fox_proj_kernel,
        grid=(rows // tm, ncb),
        in_specs=[pl.BlockSpec((tm, d), lambda i, j: (i + off, 0)),
                  w_spec(0), w_spec(1), w_spec(2),
                  pl.BlockSpec((d, V7X_LANES), lambda i, j: (0, 0)),
                  pl.BlockSpec((1, V7X_LANES), lambda i, j: (0, 0))],
        out_specs=[out_spec, out_spec, out_spec,
                   pl.BlockSpec((tm, V7X_LANES), lambda i, j: (i, 0))],
        out_shape=[jax.ShapeDtypeStruct((rows, FOX_WIDTH), BF16),
                   jax.ShapeDtypeStruct((rows, FOX_WIDTH), F32),
                   jax.ShapeDtypeStruct((rows, FOX_WIDTH), F32),
                   jax.ShapeDtypeStruct((rows, V7X_LANES), F32)],
        compiler_params=_cparams("parallel", "arbitrary"),
        name="fox_project",
    )(x, w_qkv, w_qkv, w_qkv, w_f, b_f)


def _fox_scores(q, k, cq, ck):
    nt = (((1,), (1,)), ((), ()))
    s = lax.dot_general(q, k.astype(BF16), nt, preferred_element_type=F32) * (FOX_HDIM ** -0.5)
    return s + cq - ck


def _softmax_step(s, v, m, l, acc):
    m_new = jnp.maximum(m, jnp.max(s, axis=-1, keepdims=True))
    alpha = jnp.exp(m - m_new)
    p = jnp.exp(s - m_new)
    l_new = alpha * l + jnp.sum(p, axis=-1, keepdims=True)
    acc_new = alpha * acc + jnp.dot(p.astype(BF16), v.astype(BF16), preferred_element_type=F32)
    return m_new, l_new, acc_new


def _fox_prompt_kernel(q_ref, k_ref, v_ref, ctok_ref, crow_ref, o_ref, *, tq):
    h = pl.program_id(1)
    iq = pl.program_id(2)
    q = q_ref[...]
    lane = lax.broadcasted_iota(jnp.int32, ctok_ref.shape, 1)
    cq = jnp.sum(jnp.where(lane == h, ctok_ref[...], 0.0), axis=-1, keepdims=True)

    def kv_block(j):
        start = pl.multiple_of(j * tq, tq)
        return k_ref[pl.ds(start, tq), :], v_ref[pl.ds(start, tq), :], crow_ref[pl.ds(j, 1), :]

    def body(j, carry):
        kj, vj, ck = kv_block(j)
        return _softmax_step(_fox_scores(q, kj, cq, ck), vj, *carry)

    init = (jnp.full((tq, 1), NEG_INF, F32), jnp.zeros((tq, 1), F32), jnp.zeros((tq, FOX_HDIM), F32))
    carry = lax.fori_loop(0, iq, body, init)
    kj, vj, ck = kv_block(iq)
    s = _fox_scores(q, kj, cq, ck)
    row = lax.broadcasted_iota(jnp.int32, (tq, tq), 0)
    col = lax.broadcasted_iota(jnp.int32, (tq, tq), 1)
    s = jnp.where(row >= col, s, NEG_INF)
    m, l, acc = _softmax_step(s, vj, *carry)
    o_ref[...] = (acc / l).astype(o_ref.dtype)


def _fox_prompt_attend(q, k, v, c, *, tq=512):
    nq = SEQ // tq
    ctok = c.reshape(N_PROMPT, FOX_HEADS)
    crow = jnp.transpose(c, (0, 2, 1)).reshape(BATCH, FOX_HEADS, nq, tq)
    return pl.pallas_call(
        functools.partial(_fox_prompt_kernel, tq=tq),
        grid=(BATCH, FOX_HEADS, nq),
        in_specs=[pl.BlockSpec((tq, FOX_HDIM), lambda b, h, i: (b * nq + i, h)),
                  pl.BlockSpec((SEQ, FOX_HDIM), lambda b, h, i: (b, h)),
                  pl.BlockSpec((SEQ, FOX_HDIM), lambda b, h, i: (b, h)),
                  pl.BlockSpec((tq, FOX_HEADS), lambda b, h, i: (b * nq + i, 0)),
                  pl.BlockSpec((None, None, nq, tq), lambda b, h, i: (b, h, 0, 0))],
        out_specs=pl.BlockSpec((tq, FOX_HDIM), lambda b, h, i: (b * nq + i, h)),
        out_shape=jax.ShapeDtypeStruct((N_PROMPT, FOX_WIDTH), BF16),
        compiler_params=_cparams("parallel", "parallel", "arbitrary"),
        name="fox_prompt_attend",
    )(q, k, v, ctok, crow)


def _fox_sample_kernel(q_ref, ck_ref, cv_ref, kn_ref, vn_ref, cq_ref, cc_ref, cn_ref, o_ref,
                       m_scr, l_scr, acc_scr):
    j = pl.program_id(1)
    S = DEC_SEQ

    @pl.when(j == 0)
    def _():
        m_scr[...] = jnp.full_like(m_scr, NEG_INF)
        l_scr[...] = jnp.zeros_like(l_scr)
        acc_scr[...] = jnp.zeros_like(acc_scr)

    def step(h, k, v, ck, mask):
        sl = slice(h * FOX_HDIM, (h + 1) * FOX_HDIM)
        s = _fox_scores(q_ref[:, sl], k, cq_ref[:, h:h + 1], ck)
        if mask is not None:
            s = jnp.where(mask, s, NEG_INF)
        m, l, acc = _softmax_step(s, v, m_scr[h][:, :1], l_scr[h][:, :1], acc_scr[h])
        m_scr[h] = jnp.broadcast_to(m, (S, V7X_LANES))
        l_scr[h] = jnp.broadcast_to(l, (S, V7X_LANES))
        acc_scr[h] = acc

    for h in range(FOX_HEADS):
        sl = slice(h * FOX_HDIM, (h + 1) * FOX_HDIM)
        step(h, ck_ref[:, sl], cv_ref[:, sl], cc_ref[h:h + 1, :], None)

    @pl.when(j == pl.num_programs(1) - 1)
    def _():
        row = lax.broadcasted_iota(jnp.int32, (S, S), 0)
        col = lax.broadcasted_iota(jnp.int32, (S, S), 1)
        for h in range(FOX_HEADS):
            sl = slice(h * FOX_HDIM, (h + 1) * FOX_HDIM)
            step(h, kn_ref[:, sl], vn_ref[:, sl], cn_ref[h:h + 1, :], row >= col)
            o_ref[:, sl] = (acc_scr[h] / l_scr[h][:, :1]).astype(o_ref.dtype)


def _fox_sample_attend(q, k_new, v_new, cache_k, cache_v, c_cache, c_new, *, layer, tk=512):
    nk = PAST_LEN // tk
    S = DEC_SEQ
    cq = c_new.reshape(N_SAMPLE, FOX_HEADS)
    cc = jnp.transpose(c_cache.reshape(DEC_BATCH, nk, tk, FOX_HEADS), (0, 1, 3, 2))
    cn = jnp.transpose(c_new, (0, 2, 1))
    row_spec = pl.BlockSpec((S, FOX_WIDTH), lambda b, j: (b, 0))
    cache_spec = pl.BlockSpec((None, None, tk, FOX_WIDTH), lambda b, j: (layer, b, j, 0))
    return pl.pallas_call(
        _fox_sample_kernel,
        grid=(DEC_BATCH, nk),
        in_specs=[row_spec, cache_spec, cache_spec, row_spec, row_spec,
                  pl.BlockSpec((S, FOX_HEADS), lambda b, j: (b, 0)),
                  pl.BlockSpec((None, None, FOX_HEADS, tk), lambda b, j: (b, j, 0, 0)),
                  pl.BlockSpec((None, FOX_HEADS, S), lambda b, j: (b, 0, 0))],
        out_specs=row_spec,
        out_shape=jax.ShapeDtypeStruct((N_SAMPLE, FOX_WIDTH), BF16),
        scratch_shapes=[pltpu.VMEM((FOX_HEADS, S, V7X_LANES), F32),
                        pltpu.VMEM((FOX_HEADS, S, V7X_LANES), F32),
                        pltpu.VMEM((FOX_HEADS, S, FOX_HDIM), F32)],
        compiler_params=_cparams("parallel", "arbitrary"),
        name="fox_sample_attend",
    )(q, cache_k, cache_v, k_new, v_new, cq, cc, cn)


def _xattn_kernel(q_ref, k_ref, v_ref, o_ref):
    nt = (((1,), (1,)), ((), ()))
    for h in range(MEM_HEADS):
        sl = slice(h * MEM_HDIM, (h + 1) * MEM_HDIM)
        s = lax.dot_general(q_ref[:, sl], k_ref[:, sl].astype(BF16), nt,
                            preferred_element_type=F32) * (MEM_HDIM ** -0.5)
        m = jnp.max(s, axis=-1, keepdims=True)
        p = jnp.exp(s - m)
        l = jnp.sum(p, axis=-1, keepdims=True)
        o = jnp.dot(p.astype(BF16), v_ref[:, sl].astype(BF16), preferred_element_type=F32)
        o_ref[:, sl] = (o / l).astype(o_ref.dtype)


def _cross_attend(q, mem_k, mem_v, *, batch, seq, row_off, k_col, v_col, tq):
    nq = seq // tq
    off = row_off // tq
    return pl.pallas_call(
        _xattn_kernel,
        grid=(batch, nq),
        in_specs=[pl.BlockSpec((tq, MEM_WIDTH), lambda b, i: (off + b * nq + i, 0)),
                  pl.BlockSpec((MEM_TOKENS, MEM_WIDTH), lambda b, i: (b, k_col)),
                  pl.BlockSpec((MEM_TOKENS, MEM_WIDTH), lambda b, i: (b, v_col))],
        out_specs=pl.BlockSpec((tq, MEM_WIDTH), lambda b, i: (b * nq + i, 0)),
        out_shape=jax.ShapeDtypeStruct((batch * seq, MEM_WIDTH), BF16),
        compiler_params=_cparams("parallel", "parallel"),
        name="cross_attend",
    )(q, mem_k, mem_v)


def _router_kernel(x_ref, w_ref, b_ref, e_ref, g_ref, r_ref, cnt_ref, cnt_scr, *, tm):
    @pl.when(pl.program_id(0) == 0)
    def _():
        cnt_scr[...] = jnp.zeros_like(cnt_scr)

    logits = jnp.dot(x_ref[...].astype(BF16), w_ref[...], preferred_element_type=F32) + b_ref[...]
    lane = lax.broadcasted_iota(jnp.int32, (tm, N_EXPERTS), 1).astype(F32)
    work = logits
    top_e, top_v = [], []
    for _ in range(TOP_K):
        m = jnp.max(work, axis=-1, keepdims=True)
        idx = jnp.min(jnp.where(work == m, lane, float(N_EXPERTS)), axis=-1, keepdims=True)
        top_e.append(idx)
        top_v.append(m)
        work = jnp.where(lane == idx, -jnp.inf, work)
    ex = [jnp.exp(v - top_v[0]) for v in top_v]
    den = ex[0] + ex[1] + ex[2] + ex[3]
    hot = jnp.zeros((tm, N_EXPERTS), F32)
    for idx in top_e:
        hot = hot + jnp.where(lane == idx, 1.0, 0.0)
    row = lax.broadcasted_iota(jnp.int32, (tm, tm), 0)
    col = lax.broadcasted_iota(jnp.int32, (tm, tm), 1)
    earlier = jnp.where(row > col, 1.0, 0.0).astype(BF16)
    before = jnp.dot(earlier, hot.astype(BF16), preferred_element_type=F32)
    before = before + cnt_scr[...]
    out_lane = lax.broadcasted_iota(jnp.int32, (tm, V7X_LANES), 1)
    e_out = jnp.zeros((tm, V7X_LANES), jnp.int32)
    g_out = jnp.zeros((tm, V7X_LANES), F32)
    r_out = jnp.zeros((tm, V7X_LANES), jnp.int32)
    for k in range(TOP_K):
        rank = jnp.sum(jnp.where(lane == top_e[k], before, 0.0), axis=-1, keepdims=True)
        e_out = jnp.where(out_lane == k, top_e[k].astype(jnp.int32), e_out)
        g_out = jnp.where(out_lane == k, ex[k] / den, g_out)
        r_out = jnp.where(out_lane == k, rank.astype(jnp.int32), r_out)
    e_ref[...] = e_out
    g_ref[...] = g_out
    r_ref[...] = r_out
    cnt_scr[...] = cnt_scr[...] + jnp.sum(hot, axis=0, keepdims=True)
    cnt_ref[...] = cnt_scr[...]


def _route(x, w_router, b_router, *, tm=512):
    rows, d = x.shape
    tok_spec = pl.BlockSpec((tm, V7X_LANES), lambda i: (i, 0))
    e, g, r, cnt = pl.pallas_call(
        functools.partial(_router_kernel, tm=tm),
        grid=(rows // tm,),
        in_specs=[pl.BlockSpec((tm, d), lambda i: (i, 0)),
                  pl.BlockSpec((d, N_EXPERTS), lambda i: (0, 0)),
                  pl.BlockSpec((1, N_EXPERTS), lambda i: (0, 0))],
        out_specs=[tok_spec, tok_spec, tok_spec, pl.BlockSpec((1, N_EXPERTS), lambda i: (0, 0))],
        out_shape=[jax.ShapeDtypeStruct((rows, V7X_LANES), jnp.int32),
                   jax.ShapeDtypeStruct((rows, V7X_LANES), F32),
                   jax.ShapeDtypeStruct((rows, V7X_LANES), jnp.int32),
                   jax.ShapeDtypeStruct((1, N_EXPERTS), F32)],
        scratch_shapes=[pltpu.VMEM((1, N_EXPERTS), F32)],
        compiler_params=_cparams("arbitrary"),
        name="moe_route",
    )(x, w_router.astype(BF16), b_router.reshape(1, N_EXPERTS).astype(F32))
    return e[:, :TOP_K], g[:, :TOP_K], r[:, :TOP_K], cnt[0].astype(jnp.int32)


def _moe_up_kernel(be_ref, br_ref, nv_ref, x_ref, wg_ref, wl_ref, bg_ref, bl_ref, h_ref):
    @pl.when(pl.program_id(1) < nv_ref[0])
    def _():
        xb = x_ref[...]
        g = jnp.dot(xb, wg_ref[...].astype(BF16), preferred_element_type=F32) + bg_ref[...]
        u = jnp.dot(xb, wl_ref[...].astype(BF16), preferred_element_type=F32) + bl_ref[...]
        x_glu = jnp.minimum(g, SWIGLU_LIMIT)
        x_lin = jnp.clip(u, -SWIGLU_LIMIT, SWIGLU_LIMIT)
        h = x_glu * _sigmoid(SWIGLU_ALPHA * x_glu) * (x_lin + 1.0)
        h_ref[...] = h.astype(h_ref.dtype)


def _moe_down_kernel(be_ref, br_ref, nv_ref, h_ref, w_ref, b_ref, o_ref):
    @pl.when(pl.program_id(1) < nv_ref[0])
    def _():
        o_ref[...] = jnp.dot(h_ref[...], w_ref[...].astype(BF16), preferred_element_type=F32) + b_ref[...]


def _moe_experts(xs, blk_e, blk_row, n_valid, w_gu, b_gu, w_down, b_down, *, layer):
    rows, d = xs.shape
    nf = D_FF // MOE_TF
    b_gu4 = b_gu.reshape(DEPTH, N_EXPERTS, 1, 2 * D_FF)
    b_down4 = b_down.reshape(DEPTH, N_EXPERTS, 1, D_MODEL)
    h = pl.pallas_call(
        _moe_up_kernel,
        grid_spec=pltpu.PrefetchScalarGridSpec(
            num_scalar_prefetch=3,
            grid=(nf, MOE_BLOCKS),
            in_specs=[
                pl.BlockSpec((MOE_TM, d), lambda j, r, be, br, nv: (br[r], 0)),
                pl.BlockSpec((None, None, d, MOE_TF), lambda j, r, be, br, nv: (layer, be[r], 0, j)),
                pl.BlockSpec((None, None, d, MOE_TF), lambda j, r, be, br, nv: (layer, be[r], 0, nf + j)),
                pl.BlockSpec((None, None, 1, MOE_TF), lambda j, r, be, br, nv: (layer, be[r], 0, j)),
                pl.BlockSpec((None, None, 1, MOE_TF), lambda j, r, be, br, nv: (layer, be[r], 0, nf + j)),
            ],
            out_specs=pl.BlockSpec((MOE_TM, MOE_TF), lambda j, r, be, br, nv: (br[r], j)),
        ),
        out_shape=jax.ShapeDtypeStruct((rows, D_FF), BF16),
        compiler_params=_cparams("arbitrary", "arbitrary"),
        name="moe_up",
    )(blk_e, blk_row, n_valid, xs, w_gu, w_gu, b_gu4, b_gu4)
    nn = D_MODEL // MOE_TN
    return pl.pallas_call(
        _moe_down_kernel,
        grid_spec=pltpu.PrefetchScalarGridSpec(
            num_scalar_prefetch=3,
            grid=(nn, MOE_BLOCKS),
            in_specs=[
                pl.BlockSpec((MOE_TM, D_FF), lambda j, r, be, br, nv: (br[r], 0)),
                pl.BlockSpec((None, None, D_FF, MOE_TN), lambda j, r, be, br, nv: (layer, be[r], 0, j)),
                pl.BlockSpec((None, None, 1, MOE_TN), lambda j, r, be, br, nv: (layer, be[r], 0, j)),
            ],
            out_specs=pl.BlockSpec((MOE_TM, MOE_TN), lambda j, r, be, br, nv: (br[r], j)),
        ),
        out_shape=jax.ShapeDtypeStruct((rows, D_MODEL), F32),
        compiler_params=_cparams("arbitrary", "arbitrary"),
        name="moe_down",
    )(blk_e, blk_row, n_valid, h, w_down, b_down4)


def _moe(x, w_router, b_router, w_gu, b_gu, w_down, b_down, *, layer):
    top_e, gate, rank, counts = _route(x, w_router[layer], b_router[layer])
    padded = (counts + MOE_TM - 1) // MOE_TM * MOE_TM
    pad_end = jnp.cumsum(padded)
    pad_start = pad_end - padded
    dest = pad_start[top_e] + rank
    n_valid = pad_end[-1] // MOE_TM
    blk = jnp.arange(MOE_BLOCKS, dtype=jnp.int32)
    blk_row = jnp.minimum(blk, n_valid - 1).astype(jnp.int32)
    blk_e = jnp.minimum(jnp.searchsorted(pad_end, blk_row * MOE_TM, side="right"),
                        N_EXPERTS - 1).astype(jnp.int32)
    tok = jnp.repeat(jnp.arange(N_TOK, dtype=jnp.int32), TOP_K)
    src = jnp.zeros((MOE_BLOCKS * MOE_TM,), jnp.int32).at[dest.reshape(-1)].set(tok)
    xs = x.astype(BF16)[src]
    out = _moe_experts(xs, blk_e, blk_row, n_valid.reshape(1).astype(jnp.int32),
                       w_gu, b_gu, w_down, b_down, layer=layer)
    return jnp.sum(out[dest] * gate[:, :, None], axis=1)


def kernel(x_prompt, x_sample, state_hgrn, cache_fox_k, cache_fox_v, cache_fox_logf, cache_mem_k, cache_mem_v, mem_prompt, hg_w_qfig, hg_lb, hg_norm_g, hg_w_o, fox_w_qkv, fox_w_f, fox_b_f, fox_w_o, mem_w_q, mem_w_kv, mem_w_o, ln_g, ln_b, moe_w_router, moe_b_router, moe_w_gu, moe_b_gu, moe_w_down, moe_b_down):
    lower_bounds = jnp.cumsum(jax.nn.softmax(hg_lb.astype(F32), axis=0), axis=0)
    x = jnp.concatenate([x_prompt.reshape(N_PROMPT, D_MODEL), x_sample.reshape(N_SAMPLE, D_MODEL)], axis=0)

    w_kv_all = jnp.concatenate([mem_w_kv[i] for i in range(DEPTH)], axis=1).astype(BF16)
    mem_kv = _matmul(mem_prompt.reshape(BATCH * MEM_TOKENS, D_MODEL), w_kv_all,
                     rows=BATCH * MEM_TOKENS, row_off=0, tm=1024, tn=1024, out_dtype=F32)
    kv5 = mem_kv.reshape(BATCH, MEM_TOKENS, DEPTH, 2, MEM_HEADS, MEM_HDIM)
    p_mem_k = jnp.transpose(kv5[:, :, :, 0], (2, 0, 1, 3, 4))
    p_mem_v = jnp.transpose(kv5[:, :, :, 1], (2, 0, 1, 3, 4))

    outs = {}
    for i in range(DEPTH):
        j = i // N_MIXERS
        if i % N_MIXERS == 0:
            proj = _matmul(x, hg_w_qfig[j].astype(BF16), rows=N_TOK, row_off=0, tm=1024, tn=1024,
                           out_dtype=F32)
            og_p, st_p = _hgrn_mix(proj, lower_bounds[j], hg_norm_g[j], None,
                                   batch=BATCH, seq=SEQ, row_off=0)
            og_s, st_s = _hgrn_mix(proj, lower_bounds[j], hg_norm_g[j], state_hgrn[j],
                                   batch=DEC_BATCH, seq=DEC_SEQ, row_off=N_PROMPT)
            outs.setdefault("p_hg", []).append(st_p)
            outs.setdefault("s_hg", []).append(st_s)
            mix_in = jnp.concatenate([og_p, og_s], axis=0)
            w_o = hg_w_o[j].astype(BF16)
        else:
            w_qkv = fox_w_qkv[j].astype(BF16)
            w_f = jnp.pad(fox_w_f[j], ((0, 0), (0, V7X_LANES - FOX_HEADS))).astype(BF16)
            b_f = jnp.pad(fox_b_f[j], (0, V7X_LANES - FOX_HEADS)).reshape(1, V7X_LANES).astype(F32)
            qp, kp, vp, lfp = _fox_project(x, w_qkv, w_f, b_f, rows=N_PROMPT, row_off=0, tm=1024)
            qs, ks, vs, lfs = _fox_project(x, w_qkv, w_f, b_f, rows=N_SAMPLE, row_off=N_PROMPT, tm=1024)
            lfp = lfp[:, :FOX_HEADS].reshape(BATCH, SEQ, FOX_HEADS)
            lfs = lfs[:, :FOX_HEADS].reshape(DEC_BATCH, DEC_SEQ, FOX_HEADS)
            o_p = _fox_prompt_attend(qp, kp, vp, jnp.cumsum(lfp, axis=1))
            c_cache = jnp.cumsum(cache_fox_logf[j].astype(F32), axis=1)
            c_new = c_cache[:, -1:, :] + jnp.cumsum(lfs, axis=1)
            o_s = _fox_sample_attend(
                qs, ks, vs,
                cache_fox_k.reshape(cache_fox_k.shape[0], DEC_BATCH, PAST_LEN, FOX_WIDTH),
                cache_fox_v.reshape(cache_fox_v.shape[0], DEC_BATCH, PAST_LEN, FOX_WIDTH),
                c_cache, c_new, layer=j)
            outs.setdefault("p_fk", []).append(kp.reshape(BATCH, SEQ, FOX_HEADS, FOX_HDIM))
            outs.setdefault("p_fv", []).append(vp.reshape(BATCH, SEQ, FOX_HEADS, FOX_HDIM))
            outs.setdefault("p_fl", []).append(lfp)
            outs.setdefault("s_fk", []).append(ks.reshape(DEC_BATCH, DEC_SEQ, FOX_HEADS, FOX_HDIM))
            outs.setdefault("s_fv", []).append(vs.reshape(DEC_BATCH, DEC_SEQ, FOX_HEADS, FOX_HDIM))
            outs.setdefault("s_fl", []).append(lfs.astype(cache_fox_logf.dtype))
            mix_in = jnp.concatenate([o_p, o_s], axis=0)
            w_o = fox_w_o[j].astype(BF16)
        x = _outproj_ln(mix_in, w_o, x, ln_g[i, 0], ln_b[i, 0])

        q = _matmul(x, mem_w_q[i].astype(BF16), rows=N_TOK, row_off=0, tm=1024, tn=MEM_WIDTH, out_dtype=BF16)
        xo_p = _cross_attend(q, mem_kv, mem_kv, batch=BATCH, seq=SEQ, row_off=0,
                             k_col=2 * i, v_col=2 * i + 1, tq=512)
        xo_s = _cross_attend(q, cache_mem_k[i].reshape(DEC_BATCH * MEM_TOKENS, MEM_WIDTH),
                             cache_mem_v[i].reshape(DEC_BATCH * MEM_TOKENS, MEM_WIDTH),
                             batch=DEC_BATCH, seq=DEC_SEQ, row_off=N_PROMPT, k_col=0, v_col=0, tq=DEC_SEQ)
        x = _outproj_ln(jnp.concatenate([xo_p, xo_s], axis=0), mem_w_o[i].astype(BF16), x,
                        ln_g[i, 1], ln_b[i, 1])

        y = _moe(x, moe_w_router, moe_b_router, moe_w_gu, moe_b_gu, moe_w_down, moe_b_down, layer=i)
        if i < DEPTH - 1:
            x = _residual_ln(x, y, ln_g[i, 2], ln_b[i, 2], rows=N_TOK, row_off=0)
        else:
            yp = _residual_ln(x, y, ln_g[i, 2], ln_b[i, 2], rows=N_PROMPT, row_off=0)
            ys = _residual_ln(x, y, ln_g[i, 2], ln_b[i, 2], rows=N_SAMPLE, row_off=N_PROMPT)

    return (yp.reshape(BATCH, SEQ, D_MODEL), ys.reshape(DEC_BATCH, DEC_SEQ, D_MODEL),
            jnp.stack(outs["p_hg"]), jnp.stack(outs["p_fk"]), jnp.stack(outs["p_fv"]), jnp.stack(outs["p_fl"]),
            p_mem_k, p_mem_v,
            jnp.stack(outs["s_hg"]), jnp.stack(outs["s_fk"]), jnp.stack(outs["s_fv"]), jnp.stack(outs["s_fl"]))
```
